```python
import jax, jax.numpy as jnp
from jax import lax
import numpy as np

D_MODEL = 1024
BATCH = 8
SEQ = 2048
DEPTH = 2

HEAD_DIM = 64
N_MIXERS = 4
GROUP_WIDTH = D_MODEL // N_MIXERS
D_MIX = N_MIXERS * GROUP_WIDTH
FOX_HEADS = GROUP_WIDTH // HEAD_DIM
FOX_BLOCK = 128
CONV_CHANNELS = GROUP_WIDTH
CONV_WIDTH = 31
SWA_Q_HEADS = GROUP_WIDTH // HEAD_DIM
SWA_KV_HEADS = SWA_Q_HEADS // 2
SWA_WINDOW = 128
SGU_GROUPS = GROUP_WIDTH // HEAD_DIM
SGU_CHUNK = 128
D_FF = 11 * D_MODEL // 4
FFN_CONV_WIDTH = 3
EPS = 1e-6

FOX_COLS = 3 * GROUP_WIDTH + FOX_HEADS
CONV_COLS = 2 * CONV_CHANNELS
SWA_COLS = SWA_Q_HEADS * HEAD_DIM + 2 * SWA_KV_HEADS * HEAD_DIM
SGU_COLS = 2 * GROUP_WIDTH
IN_COLS = FOX_COLS + CONV_COLS + SWA_COLS + SGU_COLS

kernel_name = "hybrid_parallel_heads_fox_conformer_swa_sgu"


def rms_norm(x, g):
    xf = x.astype(jnp.float32)
    y = xf * lax.rsqrt(jnp.mean(xf * xf, axis=-1, keepdims=True) + EPS)
    return (y * g.astype(jnp.float32)).astype(x.dtype)


def layer_norm(x, g, b):
    xf = x.astype(jnp.float32)
    mu = jnp.mean(xf, axis=-1, keepdims=True)
    xc = xf - mu
    var = jnp.mean(xc * xc, axis=-1, keepdims=True)
    y = xc * lax.rsqrt(var + EPS) * g.astype(jnp.float32) + b.astype(jnp.float32)
    return y.astype(x.dtype)


def causal_depthwise_conv(x, w, b):
    k_width, ch = w.shape
    y = lax.conv_general_dilated(
        x, w[:, None, :].astype(x.dtype), window_strides=(1,), padding=[(k_width - 1, 0)],
        dimension_numbers=('NWC', 'WIO', 'NWC'), feature_group_count=ch)
    return y + b.astype(x.dtype)


def fox_attention(q, k, v, fg_logit, b_f):
    bsz, seq, heads, dh = q.shape
    log_f = jax.nn.log_sigmoid(fg_logit.astype(jnp.float32) + b_f.astype(jnp.float32))
    cum = jnp.cumsum(log_f, axis=1)
    nb = seq // FOX_BLOCK
    qb = q.reshape(bsz, nb, FOX_BLOCK, heads, dh).transpose(1, 0, 2, 3, 4)
    cb = cum.reshape(bsz, nb, FOX_BLOCK, heads).transpose(1, 0, 2, 3)
    cum_k = cum.transpose(0, 2, 1)
    key_pos = jnp.arange(seq)
    scale = HEAD_DIM ** -0.5

    def block(args):
        q_blk, c_blk, i = args
        s = jnp.einsum('bqhd,bkhd->bhqk', q_blk, k, preferred_element_type=jnp.float32) * scale
        s = s + c_blk.transpose(0, 2, 1)[..., None] - cum_k[:, :, None, :]
        q_pos = i * FOX_BLOCK + jnp.arange(FOX_BLOCK)
        s = jnp.where(key_pos[None, :] <= q_pos[:, None], s, -jnp.inf)
        p = jax.nn.softmax(s, axis=-1).astype(v.dtype)
        return jnp.einsum('bhqk,bkhd->bqhd', p, v)

    out = lax.map(block, (qb, cb, jnp.arange(nb)))
    return out.transpose(1, 0, 2, 3, 4).reshape(bsz, seq, heads * dh)


def swa_attention(q, k, v, sinks):
    bsz, seq, hq, dh = q.shape
    hkv = k.shape[2]
    grp = hq // hkv
    win = SWA_WINDOW
    nb = seq // win
    qb = q.reshape(bsz, nb, win, hkv, grp, dh)
    pad = jnp.zeros((bsz, win, hkv, dh), k.dtype)
    kp = jnp.concatenate([pad, k], axis=1).reshape(bsz, nb + 1, win, hkv, dh)
    vp = jnp.concatenate([pad.astype(v.dtype), v], axis=1).reshape(bsz, nb + 1, win, hkv, dh)
    kw = jnp.concatenate([kp[:, :-1], kp[:, 1:]], axis=2)
    vw = jnp.concatenate([vp[:, :-1], vp[:, 1:]], axis=2)
    s = jnp.einsum('bnqhgd,bnkhd->bnhgqk', qb, kw, preferred_element_type=jnp.float32) * (HEAD_DIM ** -0.5)
    qi = jnp.arange(win)[:, None]
    kj = jnp.arange(2 * win)[None, :]
    blk = jnp.arange(nb)[:, None, None]
    mask = (kj > qi) & (kj <= qi + win) & (blk * win + kj >= win)
    s = jnp.where(mask[None, :, None, None], s, -jnp.inf)
    sink = sinks.astype(jnp.float32).reshape(hkv, grp)[None, None, :, :, None, None]
    m = jnp.maximum(jnp.max(s, axis=-1, keepdims=True), sink)
    p = jnp.exp(s - m)
    p = p / (jnp.sum(p, axis=-1, keepdims=True) + jnp.exp(sink - m))
    out = jnp.einsum('bnhgqk,bnkhd->bnqhgd', p.astype(v.dtype), vw)
    return out.reshape(bsz, seq, hq * dh)


def conformer_conv(z, conv_w, conv_b, ln_g, ln_b, pw_w, pw_b):
    a, g = jnp.split(z, 2, axis=-1)
    h = a * jax.nn.sigmoid(g)
    h = causal_depthwise_conv(h, conv_w, conv_b)
    h = jax.nn.silu(layer_norm(h, ln_g, ln_b))
    return h @ pw_w + pw_b


def spatial_gating(z, ln_g, ln_b, w_s, b_s):
    z = jax.nn.gelu(z)
    u, v = jnp.split(z, 2, axis=-1)
    v = layer_norm(v, ln_g, ln_b)
    bsz, seq, _ = v.shape
    nc = seq // SGU_CHUNK
    gd = GROUP_WIDTH // SGU_GROUPS
    v = v.reshape(bsz, nc, SGU_CHUNK, SGU_GROUPS, gd)
    tri = jnp.tril(jnp.ones((SGU_CHUNK, SGU_CHUNK), dtype=bool))
    w = jnp.where(tri[None], w_s, 0).astype(v.dtype)
    mix = jnp.einsum('gts,bnsgc->bntgc', w, v) + b_s.T.astype(v.dtype)[:, :, None]
    return u * mix.reshape(bsz, seq, GROUP_WIDTH)


def hybrid_mixer(h, w_in, b_fgate, conv_w, conv_b, conv_ln_g, conv_ln_b, conv_pw_w, conv_pw_b,
                 swa_sinks, sgu_ln_g, sgu_ln_b, sgu_w, sgu_b, g_group, w_out):
    bsz, seq, _ = h.shape
    z = h @ w_in
    z_fox, z_conv, z_swa, z_sgu = jnp.split(
        z, [FOX_COLS, FOX_COLS + CONV_COLS, FOX_COLS + CONV_COLS + SWA_COLS], axis=-1)
    fq, fk, fv, ff = jnp.split(z_fox, [GROUP_WIDTH, 2 * GROUP_WIDTH, 3 * GROUP_WIDTH], axis=-1)
    hs = (bsz, seq, FOX_HEADS, HEAD_DIM)
    y_fox = fox_attention(fq.reshape(hs), fk.reshape(hs), fv.reshape(hs), ff, b_fgate)
    y_conv = conformer_conv(z_conv, conv_w, conv_b, conv_ln_g, conv_ln_b, conv_pw_w, conv_pw_b)
    nq = SWA_Q_HEADS * HEAD_DIM
    nkv = SWA_KV_HEADS * HEAD_DIM
    sq, sk, sv = jnp.split(z_swa, [nq, nq + nkv], axis=-1)
    y_swa = swa_attention(sq.reshape(bsz, seq, SWA_Q_HEADS, HEAD_DIM),
                          sk.reshape(bsz, seq, SWA_KV_HEADS, HEAD_DIM),
                          sv.reshape(bsz, seq, SWA_KV_HEADS, HEAD_DIM), swa_sinks)
    y_sgu = spatial_gating(z_sgu, sgu_ln_g, sgu_ln_b, sgu_w, sgu_b)
    y = jnp.stack([y_fox, y_conv, y_swa, y_sgu], axis=2)
    y = rms_norm(y, g_group.reshape(N_MIXERS, GROUP_WIDTH))
    return y.reshape(bsz, seq, D_MIX) @ w_out


def conv_ffn(h, w_up, conv_w, conv_b, w_down):
    u = causal_depthwise_conv(h @ w_up, conv_w, conv_b)
    g, val = jnp.split(u, 2, axis=-1)
    return (jax.nn.silu(g) * val) @ w_down


def setup_inputs(seed: int = 0) -> dict:
    key = jax.random.key(seed)
    ks = jax.random.split(key, 32)
    L = DEPTH

    def nrm(k, shape, scale):
        return jax.random.normal(k, shape, jnp.float32) * scale

    return {
        "x": nrm(ks[0], (BATCH, SEQ, D_MODEL), 1.0),
        "c": nrm(ks[1], (BATCH, D_MODEL), 1.0),
        "w_ada": nrm(ks[2], (L, D_MODEL, 6 * D_MODEL), 0.5 * D_MODEL ** -0.5),
        "b_ada": nrm(ks[3], (L, 6 * D_MODEL), 0.02),
        "g_pre_mix": 1.0 + nrm(ks[4], (L, D_MODEL), 0.05),
        "g_post_mix": 1.0 + nrm(ks[5], (L, D_MODEL), 0.05),
        "g_pre_ffn": 1.0 + nrm(ks[6], (L, D_MODEL), 0.05),
        "g_post_ffn": 1.0 + nrm(ks[7], (L, D_MODEL), 0.05),
        "w_in": nrm(ks[8], (L, D_MODEL, IN_COLS), D_MODEL ** -0.5),
        "b_fgate": 2.0 + nrm(ks[9], (L, FOX_HEADS), 0.1),
        "conv_w": nrm(ks[10], (L, CONV_WIDTH, CONV_CHANNELS), CONV_WIDTH ** -0.5),
        "conv_b": nrm(ks[11], (L, CONV_CHANNELS), 0.02),
        "conv_ln_g": 1.0 + nrm(ks[12], (L, CONV_CHANNELS), 0.05),
        "conv_ln_b": nrm(ks[13], (L, CONV_CHANNELS), 0.02),
        "conv_pw_w": nrm(ks[14], (L, CONV_CHANNELS, CONV_CHANNELS), CONV_CHANNELS ** -0.5),
        "conv_pw_b": nrm(ks[15], (L, CONV_CHANNELS), 0.02),
        "swa_sinks": nrm(ks[16], (L, SWA_Q_HEADS), 0.5),
        "sgu_ln_g": 1.0 + nrm(ks[17], (L, GROUP_WIDTH), 0.05),
        "sgu_ln_b": nrm(ks[18], (L, GROUP_WIDTH), 0.02),
        "sgu_w": nrm(ks[19], (L, SGU_GROUPS, SGU_CHUNK, SGU_CHUNK), SGU_CHUNK ** -0.5),
        "sgu_b": 1.0 + nrm(ks[20], (L, SGU_GROUPS, SGU_CHUNK), 0.05),
        "g_group": 1.0 + nrm(ks[21], (L, D_MIX), 0.05),
        "w_out": nrm(ks[22], (L, D_MIX, D_MODEL), D_MIX ** -0.5),
        "ffn_w_up": nrm(ks[23], (L, D_MODEL, 2 * D_FF), D_MODEL ** -0.5),
        "ffn_conv_w": nrm(ks[24], (L, FFN_CONV_WIDTH, 2 * D_FF), FFN_CONV_WIDTH ** -0.5),
        "ffn_conv_b": nrm(ks[25], (L, 2 * D_FF), 0.02),
        "ffn_w_down": nrm(ks[26], (L, D_FF, D_MODEL), D_FF ** -0.5),
    }


def reference(x, c, w_ada, b_ada, g_pre_mix, g_post_mix, g_pre_ffn, g_post_ffn, w_in, b_fgate,
              conv_w, conv_b, conv_ln_g, conv_ln_b, conv_pw_w, conv_pw_b, swa_sinks,
              sgu_ln_g, sgu_ln_b, sgu_w, sgu_b, g_group, w_out,
              ffn_w_up, ffn_conv_w, ffn_conv_b, ffn_w_down):
    c_act = jax.nn.silu(c)
    for l in range(DEPTH):
        mod = (c_act @ w_ada[l] + b_ada[l])[:, None, :]
        sh1, sc1, ga1, sh2, sc2, ga2 = jnp.split(mod, 6, axis=-1)
        h = rms_norm(x, g_pre_mix[l]) * (1.0 + sc1) + sh1
        y = hybrid_mixer(h, w_in[l], b_fgate[l], conv_w[l], conv_b[l], conv_ln_g[l], conv_ln_b[l],
                         conv_pw_w[l], conv_pw_b[l], swa_sinks[l], sgu_ln_g[l], sgu_ln_b[l],
                         sgu_w[l], sgu_b[l], g_group[l], w_out[l])
        x = x + ga1 * rms_norm(y, g_post_mix[l])
        h = rms_norm(x, g_pre_ffn[l]) * (1.0 + sc2) + sh2
        y = conv_ffn(h, ffn_w_up[l], ffn_conv_w[l], ffn_conv_b[l], ffn_w_down[l])
        x = x + ga2 * rms_norm(y, g_post_ffn[l])
    return x
```

```python
import functools

import jax
import jax.numpy as jnp
from jax import lax
from jax.experimental import pallas as pl
from jax.experimental.pallas import tpu as pltpu

D_MODEL = 1024
BATCH = 8
SEQ = 2048
DEPTH = 2
HEAD_DIM = 64
GROUP_WIDTH = 256
N_HEADS = 4
CONV_WIDTH = 31
SWA_WINDOW = 128
SGU_CHUNK = 128
D_FF = 2816
EPS = 1e-6
SCALE = HEAD_DIM ** -0.5

LANES = 128
SUBLANES = 8
HALF = HEAD_DIM

T_MIX = 256
T_FFN = 512
FF_CHUNK = 256
N_FF_CHUNKS = D_FF // FF_CHUNK
CONV_HALO = 32
VMEM_LIMIT = 56 * 1024 * 1024

C_FQ, C_FK, C_FV = 0, 256, 512
C_CA, C_CG = 768, 1024
C_SQ, C_SK, C_SV = 1280, 1536, 1664
C_GU, C_GV = 1792, 2048
IN_COLS_ALIGNED = 2304
SWA_HEAD_ORDER = (0, 2, 1, 3)

F32 = jnp.float32
BF16 = jnp.bfloat16


def _mm(a, b):
    return jnp.dot(a, b, preferred_element_type=F32)


def _mm_nt(a, b):
    return lax.dot_general(a, b, (((1,), (1,)), ((), ())), preferred_element_type=F32)


def _rms(xf, g):
    return xf * lax.rsqrt(jnp.mean(xf * xf, axis=-1, keepdims=True) + EPS) * g


def _layer_norm(xf, g, b):
    mu = jnp.mean(xf, axis=-1, keepdims=True)
    xc = xf - mu
    var = jnp.mean(xc * xc, axis=-1, keepdims=True)
    return xc * lax.rsqrt(var + EPS) * g + b


def _sigmoid(x):
    return 1.0 / (1.0 + jnp.exp(-x))


def _ada_kernel(c_ref, w_ref, b_ref, o_ref):
    c = c_ref[...]
    c_act = (c * _sigmoid(c)).astype(BF16)
    o_ref[0] = _mm(c_act, w_ref[0].astype(BF16)) + b_ref[0]


def _ada(c, w_ada, b_ada):
    n_col = 6 * D_MODEL // D_MODEL
    return pl.pallas_call(
        _ada_kernel,
        grid=(DEPTH, n_col),
        in_specs=[
            pl.BlockSpec((BATCH, D_MODEL), lambda l, j: (0, 0)),
            pl.BlockSpec((1, D_MODEL, D_MODEL), lambda l, j: (l, 0, j)),
            pl.BlockSpec((1, 1, D_MODEL), lambda l, j: (l, 0, j)),
        ],
        out_specs=pl.BlockSpec((1, BATCH, D_MODEL), lambda l, j: (l, 0, j)),
        out_shape=jax.ShapeDtypeStruct((DEPTH, BATCH, 6 * D_MODEL), F32),
        compiler_params=pltpu.CompilerParams(
            dimension_semantics=("arbitrary", "arbitrary")),
        name="ada",
    )(c, w_ada, b_ada.reshape(DEPTH, 1, 6 * D_MODEL))


def _mixer_kernel(x_ref, mod_ref, gpre_ref, win_ref, wgt_ref, bf_ref,
                  cw_ref, cb_ref, clg_ref, clb_ref, pww_ref, pwb_ref,
                  sinks_ref, slg_ref, slb_ref, sw_ref, sb_ref,
                  gg_ref, wout_ref, gpost_ref, out_ref,
                  q_scr, kf_scr, vf_scr, frow_scr, fcar_scr, gcar_scr,
                  ks_scr, vs_scr, m_scr, l_scr, acc_scr, ycat_scr):
    T = T_MIX
    i = pl.program_id(1)
    row0 = pl.multiple_of(i * T, T)

    @pl.when(i == 0)
    def _():
        fcar_scr[...] = jnp.zeros_like(fcar_scr)
        gcar_scr[...] = jnp.zeros_like(gcar_scr)
        ks_scr[0:SWA_WINDOW, :] = jnp.zeros((SWA_WINDOW, LANES), BF16)
        vs_scr[0:SWA_WINDOW, :] = jnp.zeros((SWA_WINDOW, LANES), BF16)

    x = x_ref[0]
    mod = mod_ref[0]
    sh1, sc1, ga1 = mod[0:1], mod[1:2], mod[2:3]
    h = (_rms(x, gpre_ref[...]) * (1.0 + sc1) + sh1).astype(BF16)

    lane = lax.broadcasted_iota(jnp.int32, (1, LANES), 1)
    lo_half = lane < HALF

    zg = _mm_nt(wgt_ref[...], h)
    xg = zg + jnp.concatenate([bf_ref[...]] * (T // LANES), axis=1)
    logf = jnp.minimum(xg, 0.0) - jnp.log1p(jnp.exp(-jnp.abs(xg)))
    p_hi = logf.astype(BF16)
    rem = logf - p_hi.astype(F32)
    p_mid = rem.astype(BF16)
    p_lo = (rem - p_mid.astype(F32)).astype(BF16)
    src = lax.broadcasted_iota(jnp.int32, (T, T), 0)
    dst = lax.broadcasted_iota(jnp.int32, (T, T), 1)
    upper = jnp.where(src <= dst, 1.0, 0.0).astype(BF16)
    csum = _mm(jnp.concatenate([p_hi, p_mid, p_lo], axis=0), upper)
    f_tile = (csum[0:8] + csum[16:24] + csum[32:40]
              + jnp.concatenate([fcar_scr[...]] * (T // LANES), axis=1))
    frow_scr[:, pl.ds(row0, T)] = f_tile
    fcar_scr[...] = jnp.broadcast_to(f_tile[:, T - 1:T], (SUBLANES, LANES))

    qf = _mm(h, win_ref[:, C_FQ:C_FQ + GROUP_WIDTH]) * SCALE
    kf_scr[pl.ds(row0, T), :] = _mm(h, win_ref[:, C_FK:C_FK + GROUP_WIDTH]).astype(BF16)
    vf_scr[pl.ds(row0, T), :] = _mm(h, win_ref[:, C_FV:C_FV + GROUP_WIDTH]).astype(BF16)
    for hh in range(N_HEADS):
        slab = qf[:, (hh // 2) * LANES:(hh // 2 + 1) * LANES]
        keep = lo_half if hh % 2 == 0 else jnp.logical_not(lo_half)
        q_scr[hh] = jnp.where(keep, slab, 0.0).astype(BF16)
    m_scr[...] = jnp.full(m_scr.shape, -jnp.inf, F32)
    l_scr[...] = jnp.zeros_like(l_scr)
    acc_scr[...] = jnp.zeros_like(acc_scr)

    qpos = lax.broadcasted_iota(jnp.int32, (T, T), 0)
    kpos = lax.broadcasted_iota(jnp.int32, (T, T), 1)
    causal = kpos <= qpos

    def fox_tile(r, diagonal):
        kt = kf_scr[pl.ds(r, T), :]
        vt = vf_scr[pl.ds(r, T), :]
        fr = frow_scr[:, pl.ds(r, T)]
        for hh in range(N_HEADS):
            sl = slice((hh // 2) * LANES, (hh // 2 + 1) * LANES)
            s = _mm_nt(q_scr[hh], kt[:, sl]) - fr[hh:hh + 1, :]
            if diagonal:
                s = jnp.where(causal, s, -jnp.inf)
            m_prev = m_scr[hh]
            m_new = jnp.maximum(m_prev, jnp.max(s, axis=1, keepdims=True))
            alpha = jnp.exp(m_prev - m_new)
            p = jnp.exp(s - m_new[:, 0:1])
            l_scr[hh] = alpha * l_scr[hh] + jnp.sum(p, axis=1, keepdims=True)
            acc_scr[hh] = alpha * acc_scr[hh] + _mm(p.astype(BF16), vt[:, sl])
            m_scr[hh] = m_new

    def fox_body(j, carry):
        fox_tile(pl.multiple_of(j * T, T), False)
        return carry

    lax.fori_loop(0, i, fox_body, 0)
    fox_tile(row0, True)

    def group_norm_store(y, g):
        yn = y * lax.rsqrt(jnp.mean(y * y, axis=-1, keepdims=True) + EPS)
        ycat_scr[:, g * GROUP_WIDTH:(g + 1) * GROUP_WIDTH] = (
            yn * gg_ref[:, g * GROUP_WIDTH:(g + 1) * GROUP_WIDTH]).astype(BF16)

    heads = [acc_scr[hh] / l_scr[hh] for hh in range(N_HEADS)]
    y_fox = jnp.concatenate([jnp.where(lo_half, heads[0], heads[1]),
                             jnp.where(lo_half, heads[2], heads[3])], axis=1)
    group_norm_store(y_fox, 0)

    ca = _mm(h, win_ref[:, C_CA:C_CA + GROUP_WIDTH])
    cg = _mm(h, win_ref[:, C_CG:C_CG + GROUP_WIDTH])
    glu = ca * _sigmoid(cg)
    ext = jnp.concatenate([gcar_scr[...], glu], axis=0)
    gcar_scr[...] = glu[T - CONV_HALO:, :]
    conv = jnp.zeros((T, GROUP_WIDTH), F32) + cb_ref[...]
    for r in range(SUBLANES):
        rolled = ext if r == 0 else pltpu.roll(ext, r, 0)
        for a in range(CONV_HALO // SUBLANES):
            back = SUBLANES * a + r
            if back > CONV_WIDTH - 1:
                continue
            k = CONV_WIDTH - 1 - back
            start = CONV_HALO - SUBLANES * a
            conv = conv + cw_ref[k:k + 1, :] * rolled[start:start + T, :]
    cn = _layer_norm(conv, clg_ref[...], clb_ref[...])
    cs = (cn * _sigmoid(cn)).astype(BF16)
    y_conv = _mm(cs, pww_ref[...]) + pwb_ref[...]
    group_norm_store(y_conv, 1)

    qs = (_mm(h, win_ref[:, C_SQ:C_SQ + GROUP_WIDTH]) * SCALE)
    ks_scr[pl.ds(SWA_WINDOW + row0, T), :] = _mm(h, win_ref[:, C_SK:C_SK + LANES]).astype(BF16)
    vs_scr[pl.ds(SWA_WINDOW + row0, T), :] = _mm(h, win_ref[:, C_SV:C_SV + LANES]).astype(BF16)
    W = SWA_WINDOW
    qi = lax.broadcasted_iota(jnp.int32, (W, 2 * W), 0)
    kj = lax.broadcasted_iota(jnp.int32, (W, 2 * W), 1)
    band = jnp.logical_and(kj > qi, kj <= qi + W)
    swa_rows = []
    for nb in range(T // W):
        base = pl.multiple_of(row0 + nb * W, W)
        kwin = ks_scr[pl.ds(base, 2 * W), :]
        vwin = vs_scr[pl.ds(base, 2 * W), :]
        keep_mask = jnp.logical_and(band, kj + base >= W)
        outs = []
        for pos, hh in enumerate(SWA_HEAD_ORDER):
            slab = qs[nb * W:(nb + 1) * W, (pos // 2) * LANES:(pos // 2 + 1) * LANES]
            keep = lo_half if pos % 2 == 0 else jnp.logical_not(lo_half)
            q_h = jnp.where(keep, slab, 0.0).astype(BF16)
            s = jnp.where(keep_mask, _mm_nt(q_h, kwin), -jnp.inf)
            sink = sinks_ref[hh]
            m = jnp.maximum(jnp.max(s, axis=1, keepdims=True), sink)
            p = jnp.exp(s - m)
            den = jnp.sum(p, axis=1, keepdims=True) + jnp.exp(sink - m)
            outs.append(_mm(p.astype(BF16), vwin) / den)
        swa_rows.append(jnp.concatenate([jnp.where(lo_half, outs[0], outs[1]),
                                         jnp.where(lo_half, outs[2], outs[3])], axis=1))
    group_norm_store(jnp.concatenate(swa_rows, axis=0), 2)

    gu = jax.nn.gelu(_mm(h, win_ref[:, C_GU:C_GU + GROUP_WIDTH]))
    gv = jax.nn.gelu(_mm(h, win_ref[:, C_GV:C_GV + GROUP_WIDTH]))
    gvn = _layer_norm(gv, slg_ref[...], slb_ref[...]).astype(BF16)
    tt = lax.broadcasted_iota(jnp.int32, (SGU_CHUNK, SGU_CHUNK), 0)
    ss = lax.broadcasted_iota(jnp.int32, (SGU_CHUNK, SGU_CHUNK), 1)
    tri = ss <= tt
    w_tri = [jnp.where(tri, sw_ref[g], 0.0).astype(BF16) for g in range(N_HEADS)]
    mix_rows = []
    for nc in range(T // SGU_CHUNK):
        vch = gvn[nc * SGU_CHUNK:(nc + 1) * SGU_CHUNK, :]
        slabs = []
        for sl in range(GROUP_WIDTH // LANES):
            vs = vch[:, sl * LANES:(sl + 1) * LANES]
            slabs.append(jnp.where(lo_half, _mm(w_tri[2 * sl], vs), _mm(w_tri[2 * sl + 1], vs)))
        mix_rows.append(jnp.concatenate(slabs, axis=1) + sb_ref[...])
    y_sgu = gu * jnp.concatenate(mix_rows, axis=0)
    group_norm_store(y_sgu, 3)

    y = _mm(ycat_scr[...], wout_ref[...])
    out_ref[0] = x + ga1 * _rms(y, gpost_ref[...])


def _mixer(x, mod, gpre, win, wgt, bfp, cw, cb, clg, clb, pww, pwb, sinks,
           slg, slb, sw, sb, gg, wout, gpost):
    T = T_MIX
    n_t = SEQ // T

    def full(shape):
        return pl.BlockSpec(shape, lambda b, i: (0,) * len(shape))

    in_specs = [
        pl.BlockSpec((1, T, D_MODEL), lambda b, i: (b, i, 0)),
        pl.BlockSpec((1, 6, D_MODEL), lambda b, i: (b, 0, 0)),
        full((1, D_MODEL)),
        full((D_MODEL, IN_COLS_ALIGNED)),
        full((2 * SUBLANES, D_MODEL)),
        full((2 * SUBLANES, LANES)),
        full((CONV_WIDTH, GROUP_WIDTH)),
        full((1, GROUP_WIDTH)),
        full((1, GROUP_WIDTH)),
        full((1, GROUP_WIDTH)),
        full((GROUP_WIDTH, GROUP_WIDTH)),
        full((1, GROUP_WIDTH)),
        pl.BlockSpec(memory_space=pltpu.SMEM),
        full((1, GROUP_WIDTH)),
        full((1, GROUP_WIDTH)),
        full((N_HEADS, SGU_CHUNK, SGU_CHUNK)),
        full((SGU_CHUNK, GROUP_WIDTH)),
        full((1, D_MODEL)),
        full((D_MODEL, D_MODEL)),
        full((1, D_MODEL)),
    ]
    scratch = [
        pltpu.VMEM((N_HEADS, T, LANES), BF16),
        pltpu.VMEM((SEQ, GROUP_WIDTH), BF16),
        pltpu.VMEM((SEQ, GROUP_WIDTH), BF16),
        pltpu.VMEM((SUBLANES, SEQ), F32),
        pltpu.VMEM((SUBLANES, LANES), F32),
        pltpu.VMEM((CONV_HALO, GROUP_WIDTH), F32),
        pltpu.VMEM((SEQ + SWA_WINDOW, LANES), BF16),
        pltpu.VMEM((SEQ + SWA_WINDOW, LANES), BF16),
        pltpu.VMEM((N_HEADS, T, LANES), F32),
        pltpu.VMEM((N_HEADS, T, LANES), F32),
        pltpu.VMEM((N_HEADS, T, LANES), F32),
        pltpu.VMEM((T, D_MODEL), BF16),
    ]
    return pl.pallas_call(
        _mixer_kernel,
        grid=(BATCH, n_t),
        in_specs=in_specs,
        out_specs=pl.BlockSpec((1, T, D_MODEL), lambda b, i: (b, i, 0)),
        out_shape=jax.ShapeDtypeStruct((BATCH, SEQ, D_MODEL), F32),
        scratch_shapes=scratch,
        compiler_params=pltpu.CompilerParams(
            dimension_semantics=("arbitrary", "arbitrary"),
            vmem_limit_bytes=VMEM_LIMIT),
        name="mixer",
    )(x, mod, gpre, win, wgt, bfp, cw, cb, clg, clb, pww, pwb, sinks,
      slg, slb, sw, sb, gg, wout, gpost)


def _ffn_kernel(x_ref, mod_ref, gpre_ref, wup_ref, cw_ref, cb_ref, wdn_ref, gpost_ref,
                out_ref, car_scr, act_scr):
    T = T_FFN
    i = pl.program_id(1)

    @pl.when(i == 0)
    def _():
        car_scr[...] = jnp.zeros_like(car_scr)

    x = x_ref[0]
    mod = mod_ref[0]
    sh2, sc2, ga2 = mod[3:4], mod[4:5], mod[5:6]
    h = (_rms(x, gpre_ref[...]) * (1.0 + sc2) + sh2).astype(BF16)

    def conv_half(col0, slot):
        u = _mm(h, wup_ref[:, col0:col0 + FF_CHUNK])
        ext = jnp.concatenate([car_scr[slot], u], axis=0)
        back1 = pltpu.roll(ext, 1, 0)[SUBLANES:, :]
        back2 = pltpu.roll(ext, 2, 0)[SUBLANES:, :]
        car_scr[slot] = u[T - SUBLANES:, :]
        w = cw_ref[:, col0:col0 + FF_CHUNK]
        return w[2:3] * u + w[1:2] * back1 + w[0:1] * back2 + cb_ref[:, col0:col0 + FF_CHUNK]

    for c in range(N_FF_CHUNKS):
        g = conv_half(c * FF_CHUNK, 2 * c)
        v = conv_half(D_FF + c * FF_CHUNK, 2 * c + 1)
        act_scr[:, c * FF_CHUNK:(c + 1) * FF_CHUNK] = (g * _sigmoid(g) * v).astype(BF16)

    y = _mm(act_scr[...], wdn_ref[...])
    out_ref[0] = x + ga2 * _rms(y, gpost_ref[...])


def _ffn(x, mod, gpre, wup, cw, cb, wdn, gpost):
    T = T_FFN
    n_t = SEQ // T

    def full(shape):
        return pl.BlockSpec(shape, lambda b, i: (0,) * len(shape))

    return pl.pallas_call(
        _ffn_kernel,
        grid=(BATCH, n_t),
        in_specs=[
            pl.BlockSpec((1, T, D_MODEL), lambda b, i: (b, i, 0)),
            pl.BlockSpec((1, 6, D_MODEL), lambda b, i: (b, 0, 0)),
            full((1, D_MODEL)),
            full((D_MODEL, 2 * D_FF)),
            full((3, 2 * D_FF)),
            full((1, 2 * D_FF)),
            full((D_FF, D_MODEL)),
            full((1, D_MODEL)),
        ],
        out_specs=pl.BlockSpec((1, T, D_MODEL), lambda b, i: (b, i, 0)),
        out_shape=jax.ShapeDtypeStruct((BATCH, SEQ, D_MODEL), F32),
        scratch_shapes=[
            pltpu.VMEM((2 * N_FF_CHUNKS, SUBLANES, FF_CHUNK), F32),
            pltpu.VMEM((T, D_FF), BF16),
        ],
        compiler_params=pltpu.CompilerParams(
            dimension_semantics=("arbitrary", "arbitrary"),
            vmem_limit_bytes=VMEM_LIMIT),
        name="ffn",
    )(x, mod, gpre, wup, cw, cb, wdn, gpost)


def _align_w_in(w_in):
    o_conv = 3 * GROUP_WIDTH + N_HEADS
    o_swa = o_conv + 2 * GROUP_WIDTH
    o_sgu = o_swa + GROUP_WIDTH + 2 * LANES
    swa_q = w_in[:, :, o_swa:o_swa + GROUP_WIDTH].reshape(DEPTH, D_MODEL, N_HEADS, HEAD_DIM)
    swa_q = swa_q[:, :, list(SWA_HEAD_ORDER), :].reshape(DEPTH, D_MODEL, GROUP_WIDTH)
    aligned = jnp.concatenate([
        w_in[:, :, 0:3 * GROUP_WIDTH],
        w_in[:, :, o_conv:o_swa],
        swa_q,
        w_in[:, :, o_swa + GROUP_WIDTH:o_sgu],
        w_in[:, :, o_sgu:o_sgu + 2 * GROUP_WIDTH],
    ], axis=-1).astype(BF16)
    gate = jnp.swapaxes(w_in[:, :, 3 * GROUP_WIDTH:o_conv], 1, 2)
    gate = jnp.pad(gate, ((0, 0), (0, 2 * SUBLANES - N_HEADS), (0, 0))).astype(BF16)
    return aligned, gate


def _permute_swa_group(a, axis):
    lo, hi = 2 * GROUP_WIDTH, 3 * GROUP_WIDTH
    seg = lax.slice_in_dim(a, lo, hi, axis=axis)
    shape = seg.shape
    split = shape[:axis] + (N_HEADS, HEAD_DIM) + shape[axis + 1:]
    seg = jnp.take(seg.reshape(split), jnp.array(SWA_HEAD_ORDER), axis=axis).reshape(shape)
    return jnp.concatenate([lax.slice_in_dim(a, 0, lo, axis=axis), seg,
                            lax.slice_in_dim(a, hi, a.shape[axis], axis=axis)], axis=axis)


def kernel(x, c, w_ada, b_ada, g_pre_mix, g_post_mix, g_pre_ffn, g_post_ffn, w_in, b_fgate,
           conv_w, conv_b, conv_ln_g, conv_ln_b, conv_pw_w, conv_pw_b, swa_sinks,
           sgu_ln_g, sgu_ln_b, sgu_w, sgu_b, g_group, w_out,
           ffn_w_up, ffn_conv_w, ffn_conv_b, ffn_w_down):
    mod = _ada(c, w_ada, b_ada).reshape(DEPTH, BATCH, 6, D_MODEL)

    win, wgt = _align_w_in(w_in)
    bfp = jnp.broadcast_to(
        jnp.pad(b_fgate, ((0, 0), (0, 2 * SUBLANES - N_HEADS)))[:, :, None],
        (DEPTH, 2 * SUBLANES, LANES)).astype(F32)
    pww = conv_pw_w.astype(BF16)
    sb = jnp.repeat(jnp.swapaxes(sgu_b, 1, 2), HEAD_DIM, axis=2)
    gg = _permute_swa_group(g_group, 1)
    wout = _permute_swa_group(w_out, 1).astype(BF16)
    wup = ffn_w_up.astype(BF16)
    wdn = ffn_w_down.astype(BF16)

    def row(a, l):
        return a[l][None, :]

    for l in range(DEPTH):
        x = _mixer(x, mod[l], row(g_pre_mix, l), win[l], wgt[l], bfp[l],
                   conv_w[l], row(conv_b, l), row(conv_ln_g, l), row(conv_ln_b, l),
                   pww[l], row(conv_pw_b, l), swa_sinks[l],
                   row(sgu_ln_g, l), row(sgu_ln_b, l), sgu_w[l], sb[l],
                   row(gg, l), wout[l], row(g_post_mix, l))
        x = _ffn(x, mod[l], row(g_pre_ffn, l), wup[l], ffn_conv_w[l], row(ffn_conv_b, l),
                 wdn[l], row(g_post_ffn, l))
    return x
```

```python
import functools

import jax
import jax.numpy as jnp
from jax import lax
from jax.experimental import pallas as pl
from jax.experimental.pallas import tpu as pltpu

D_MODEL = 1024
BATCH = 8
SEQ = 2048
DEPTH = 2
HEAD_DIM = 64
GROUP_WIDTH = 256
N_HEADS = 4
CONV_WIDTH = 31
SWA_WINDOW = 128
SGU_CHUNK = 128
D_FF = 2816
EPS = 1e-6
SCALE = HEAD_DIM ** -0.5

LANES = 128
SUBLANES = 8
HALF = HEAD_DIM

T_MIX = 512
T_FFN = 512
FF_CHUNK = 256
N_FF_CHUNKS = D_FF // FF_CHUNK
CONV_HALO = 32
CONV_ROWS = 64
VMEM_LIMIT = 56 * 1024 * 1024

C_FQ, C_FK, C_FV = 0, 256, 512
C_CA, C_CG = 768, 1024
C_SQ, C_SK, C_SV = 1280, 1536, 1664
C_GU, C_GV = 1792, 2048
C_FG = 2304
IN_COLS_ALIGNED = 2432
SWA_HEAD_ORDER = (0, 2, 1, 3)

F32 = jnp.float32
BF16 = jnp.bfloat16


def _mm(a, b):
    return jnp.dot(a, b, preferred_element_type=F32)


def _mm_nt(a, b):
    return lax.dot_general(a, b, (((1,), (1,)), ((), ())), preferred_element_type=F32)


def _rms(xf, g):
    return xf * lax.rsqrt(jnp.mean(xf * xf, axis=-1, keepdims=True) + EPS) * g


def _layer_norm(xf, g, b):
    mu = jnp.mean(xf, axis=-1, keepdims=True)
    xc = xf - mu
    var = jnp.mean(xc * xc, axis=-1, keepdims=True)
    return xc * lax.rsqrt(var + EPS) * g + b


def _sigmoid(x):
    return 1.0 / (1.0 + jnp.exp(-x))


def _layer_block(shape, layer):
    return pl.BlockSpec((None,) + shape, lambda b, i: (layer,) + (0,) * len(shape))


def _ada_kernel(c_ref, w_ref, b_ref, o_ref):
    c = c_ref[...]
    c_act = (c * _sigmoid(c)).astype(BF16)
    o_ref[0] = _mm(c_act, w_ref[0].astype(BF16)) + b_ref[0]


def _ada(c, w_ada, b_ada):
    n_col = 6
    return pl.pallas_call(
        _ada_kernel,
        grid=(DEPTH, n_col),
        in_specs=[
            pl.BlockSpec((BATCH, D_MODEL), lambda l, j: (0, 0)),
            pl.BlockSpec((1, D_MODEL, D_MODEL), lambda l, j: (l, 0, j)),
            pl.BlockSpec((1, 1, D_MODEL), lambda l, j: (l, 0, j)),
        ],
        out_specs=pl.BlockSpec((1, BATCH, D_MODEL), lambda l, j: (l, 0, j)),
        out_shape=jax.ShapeDtypeStruct((DEPTH, BATCH, n_col * D_MODEL), F32),
        compiler_params=pltpu.CompilerParams(
            dimension_semantics=("arbitrary", "arbitrary")),
        name="ada",
    )(c, w_ada, b_ada.reshape(DEPTH, 1, n_col * D_MODEL))


def _mixer_kernel(x_ref, mod_ref, gpre_ref, win_ref, bf_ref,
                  cw_ref, cb_ref, clg_ref, clb_ref, pww_ref, pwb_ref,
                  sinks_ref, slg_ref, slb_ref, sw_ref, sb_ref,
                  gg_ref, wout_ref, gpost_ref, out_ref,
                  q_scr, kf_scr, va_scr, frow_scr, fcar_scr, gcar_scr,
                  ks_scr, vs_scr, m_scr, acc_scr, s_scr, ycat_scr, *, layer):
    T = T_MIX
    i = pl.program_id(1)
    row0 = pl.multiple_of(i * T, T)

    @pl.when(i == 0)
    def _():
        fcar_scr[...] = jnp.zeros_like(fcar_scr)
        gcar_scr[...] = jnp.zeros_like(gcar_scr)
        ks_scr[0:SWA_WINDOW, :] = jnp.zeros((SWA_WINDOW, LANES), BF16)
        vs_scr[0:SWA_WINDOW, :] = jnp.zeros((SWA_WINDOW, LANES), BF16)

    x = x_ref[0]
    mod = mod_ref[0]
    sh1, sc1, ga1 = mod[0:1], mod[1:2], mod[2:3]
    h = (_rms(x, gpre_ref[...]) * (1.0 + sc1) + sh1).astype(BF16)

    lane = lax.broadcasted_iota(jnp.int32, (1, LANES), 1)
    lo_half = lane < HALF
    hi_half = jnp.logical_not(lo_half)

    def group_norm_store(y, g):
        yn = y * lax.rsqrt(jnp.mean(y * y, axis=-1, keepdims=True) + EPS)
        ycat_scr[:, g * GROUP_WIDTH:(g + 1) * GROUP_WIDTH] = (
            yn * gg_ref[:, g * GROUP_WIDTH:(g + 1) * GROUP_WIDTH]).astype(BF16)

    ca = _mm(h, win_ref[:, C_CA:C_CA + GROUP_WIDTH])
    cg = _mm(h, win_ref[:, C_CG:C_CG + GROUP_WIDTH])
    glu = ca * _sigmoid(cg)
    ext = jnp.concatenate([gcar_scr[...], glu], axis=0)
    gcar_scr[...] = glu[T - CONV_HALO:, :]

    qf = _mm(h, win_ref[:, C_FQ:C_FQ + GROUP_WIDTH]) * SCALE
    kf_scr[pl.ds(row0, T), :] = _mm(h, win_ref[:, C_FK:C_FK + GROUP_WIDTH]).astype(BF16)
    vf = _mm(h, win_ref[:, C_FV:C_FV + GROUP_WIDTH])
    for hh in range(N_HEADS):
        sl = slice((hh // 2) * LANES, (hh // 2 + 1) * LANES)
        keep = lo_half if hh % 2 == 0 else hi_half
        q_scr[hh] = jnp.where(keep, qf[:, sl], 0.0).astype(BF16)
        va_scr[hh, pl.ds(row0, T), :] = jnp.where(keep, vf[:, sl], 1.0).astype(BF16)
    m_scr[...] = jnp.full(m_scr.shape, -jnp.inf, F32)
    acc_scr[...] = jnp.zeros_like(acc_scr)

    qs = (_mm(h, win_ref[:, C_SQ:C_SQ + GROUP_WIDTH]) * SCALE)
    ks_scr[pl.ds(SWA_WINDOW + row0, T), :] = _mm(h, win_ref[:, C_SK:C_SK + LANES]).astype(BF16)
    vs_scr[pl.ds(SWA_WINDOW + row0, T), :] = _mm(h, win_ref[:, C_SV:C_SV + LANES]).astype(BF16)
    zu = _mm(h, win_ref[:, C_GU:C_GU + GROUP_WIDTH])
    zv = _mm(h, win_ref[:, C_GV:C_GV + GROUP_WIDTH])

    zg = _mm(h, win_ref[:, C_FG:C_FG + LANES]).T[0:2 * SUBLANES, :]
    xg = zg + jnp.concatenate([bf_ref[...]] * (T // LANES), axis=1)
    logf = jnp.minimum(xg, 0.0) - jnp.log1p(jnp.exp(-jnp.abs(xg)))
    p_hi = logf.astype(BF16)
    rem = logf - p_hi.astype(F32)
    p_mid = rem.astype(BF16)
    p_lo = (rem - p_mid.astype(F32)).astype(BF16)
    src = lax.broadcasted_iota(jnp.int32, (T, T), 0)
    dst = lax.broadcasted_iota(jnp.int32, (T, T), 1)
    upper = jnp.where(src <= dst, 1.0, 0.0).astype(BF16)
    csum = _mm(jnp.concatenate([p_hi, p_mid, p_lo], axis=0), upper)
    f_tile = (csum[0:8] + csum[16:24] + csum[32:40]
              + jnp.concatenate([fcar_scr[...]] * (T // LANES), axis=1))
    frow_scr[:, pl.ds(row0, T)] = f_tile
    fcar_scr[...] = jnp.broadcast_to(f_tile[:, T - 1:T], (SUBLANES, LANES))

    conv_blocks = []
    for rb in range(T // CONV_ROWS):
        row_blocks = []
        for cb in range(GROUP_WIDTH // LANES):
            cols = slice(cb * LANES, (cb + 1) * LANES)
            win = ext[rb * CONV_ROWS:rb * CONV_ROWS + CONV_ROWS + CONV_HALO, cols]
            acc = jnp.zeros((CONV_ROWS, LANES), F32) + cb_ref[:, cols]
            for r in range(SUBLANES):
                rolled = win if r == 0 else pltpu.roll(win, r, 0)
                for a in range(CONV_HALO // SUBLANES):
                    back = SUBLANES * a + r
                    if back > CONV_WIDTH - 1:
                        continue
                    k = CONV_WIDTH - 1 - back
                    start = CONV_HALO - SUBLANES * a
                    acc = acc + cw_ref[k:k + 1, cols] * rolled[start:start + CONV_ROWS, :]
            row_blocks.append(acc)
        conv_blocks.append(jnp.concatenate(row_blocks, axis=1))
    conv = jnp.concatenate(conv_blocks, axis=0)
    cn = _layer_norm(conv, clg_ref[...], clb_ref[...])
    cs = (cn * _sigmoid(cn)).astype(BF16)
    y_conv = _mm(cs, pww_ref[...]) + pwb_ref[...]
    group_norm_store(y_conv, 1)

    qpos = lax.broadcasted_iota(jnp.int32, (T, T), 0)
    kpos = lax.broadcasted_iota(jnp.int32, (T, T), 1)
    causal = kpos <= qpos

    def fox_scores(r, slot):
        kt = kf_scr[pl.ds(r, T), :]
        fr = frow_scr[:, pl.ds(r, T)]
        for pair in range(N_HEADS // 2):
            q2 = q_scr[2 * pair:2 * pair + 2].reshape(2 * T, LANES)
            s2 = _mm_nt(q2, kt[:, pair * LANES:(pair + 1) * LANES])
            for e in range(2):
                hh = 2 * pair + e
                s_scr[slot, hh] = s2[e * T:(e + 1) * T, :] - fr[hh:hh + 1, :]

    def fox_accumulate(r, slot, diagonal):
        for hh in range(N_HEADS):
            s = s_scr[slot, hh]
            if diagonal:
                s = jnp.where(causal, s, -jnp.inf)
            m_prev = m_scr[hh]
            m_new = jnp.maximum(m_prev, jnp.max(s, axis=1, keepdims=True))
            alpha = jnp.exp(m_prev - m_new)
            p = jnp.exp(s - jnp.concatenate([m_new] * (T // LANES), axis=1))
            acc_scr[hh] = alpha * acc_scr[hh] + _mm(p.astype(BF16), va_scr[hh, pl.ds(r, T), :])
            m_scr[hh] = m_new

    fox_scores(0, 0)

    def fox_body(j, carry):
        fox_accumulate(pl.multiple_of(j * T, T), j & 1, False)
        fox_scores(pl.multiple_of((j + 1) * T, T), (j + 1) & 1)
        return carry

    lax.fori_loop(0, i, fox_body, 0)
    fox_accumulate(row0, i & 1, True)

    heads = []
    for hh in range(N_HEADS):
        acc = acc_scr[hh]
        heads.append(acc / pltpu.roll(acc, HALF, 1))
    y_fox = jnp.concatenate([jnp.where(lo_half, heads[0], heads[1]),
                             jnp.where(lo_half, heads[2], heads[3])], axis=1)
    group_norm_store(y_fox, 0)

    W = SWA_WINDOW
    qi = lax.broadcasted_iota(jnp.int32, (W, 2 * W), 0)
    kj = lax.broadcasted_iota(jnp.int32, (W, 2 * W), 1)
    band = jnp.logical_and(kj > qi, kj <= qi + W)
    swa_rows = []
    for nb in range(T // W):
        base = pl.multiple_of(row0 + nb * W, W)
        kwin = ks_scr[pl.ds(base, 2 * W), :]
        vwin = vs_scr[pl.ds(base, 2 * W), :]
        keep_mask = jnp.logical_and(band, kj + base >= W)
        outs = []
        for pos, hh in enumerate(SWA_HEAD_ORDER):
            slab = qs[nb * W:(nb + 1) * W, (pos // 2) * LANES:(pos // 2 + 1) * LANES]
            keep = lo_half if pos % 2 == 0 else hi_half
            q_h = jnp.where(keep, slab, 0.0).astype(BF16)
            s = jnp.where(keep_mask, _mm_nt(q_h, kwin), -jnp.inf)
            sink = sinks_ref[layer, hh]
            m = jnp.maximum(jnp.max(s, axis=1, keepdims=True), sink)
            p = jnp.exp(s - m)
            den = jnp.sum(p, axis=1, keepdims=True) + jnp.exp(sink - m)
            outs.append(_mm(p.astype(BF16), vwin) / den)
        swa_rows.append(jnp.concatenate([jnp.where(lo_half, outs[0], outs[1]),
                                         jnp.where(lo_half, outs[2], outs[3])], axis=1))
    group_norm_store(jnp.concatenate(swa_rows, axis=0), 2)

    gu = jax.nn.gelu(zu)
    gv = jax.nn.gelu(zv)
    gvn = _layer_norm(gv, slg_ref[...], slb_ref[...]).astype(BF16)
    tt = lax.broadcasted_iota(jnp.int32, (SGU_CHUNK, SGU_CHUNK), 0)
    ss = lax.broadcasted_iota(jnp.int32, (SGU_CHUNK, SGU_CHUNK), 1)
    tri = ss <= tt
    w_tri = [jnp.where(tri, sw_ref[g], 0.0).astype(BF16) for g in range(N_HEADS)]
    mix_rows = []
    for nc in range(T // SGU_CHUNK):
        vch = gvn[nc * SGU_CHUNK:(nc + 1) * SGU_CHUNK, :]
        slabs = []
        for sl in range(GROUP_WIDTH // LANES):
            vs = vch[:, sl * LANES:(sl + 1) * LANES]
            slabs.append(jnp.where(lo_half, _mm(w_tri[2 * sl], vs), _mm(w_tri[2 * sl + 1], vs)))
        mix_rows.append(jnp.concatenate(slabs, axis=1) + sb_ref[...])
    y_sgu = gu * jnp.concatenate(mix_rows, axis=0)
    group_norm_store(y_sgu, 3)

    y = _mm(ycat_scr[...], wout_ref[...])
    out_ref[0] = x + ga1 * _rms(y, gpost_ref[...])


def _mixer(layer, x, mod, gpre, win, bfp, cw, cb, clg, clb, pww, pwb, sinks,
           slg, slb, sw, sb, gg, wout, gpost):
    T = T_MIX
    n_t = SEQ // T
    blk = functools.partial(_layer_block, layer=layer)
    in_specs = [
        pl.BlockSpec((1, T, D_MODEL), lambda b, i: (b, i, 0)),
        pl.BlockSpec((None, 1, 6, D_MODEL), lambda b, i: (layer, b, 0, 0)),
        blk((1, D_MODEL)),
        blk((D_MODEL, IN_COLS_ALIGNED)),
        blk((2 * SUBLANES, LANES)),
        blk((CONV_WIDTH, GROUP_WIDTH)),
        blk((1, GROUP_WIDTH)),
        blk((1, GROUP_WIDTH)),
        blk((1, GROUP_WIDTH)),
        blk((GROUP_WIDTH, GROUP_WIDTH)),
        blk((1, GROUP_WIDTH)),
        pl.BlockSpec(memory_space=pltpu.SMEM),
        blk((1, GROUP_WIDTH)),
        blk((1, GROUP_WIDTH)),
        blk((N_HEADS, SGU_CHUNK, SGU_CHUNK)),
        blk((SGU_CHUNK, GROUP_WIDTH)),
        blk((1, D_MODEL)),
        blk((D_MODEL, D_MODEL)),
        blk((1, D_MODEL)),
    ]
    scratch = [
        pltpu.VMEM((N_HEADS, T, LANES), BF16),
        pltpu.VMEM((SEQ, GROUP_WIDTH), BF16),
        pltpu.VMEM((N_HEADS, SEQ, LANES), BF16),
        pltpu.VMEM((SUBLANES, SEQ), F32),
        pltpu.VMEM((SUBLANES, LANES), F32),
        pltpu.VMEM((CONV_HALO, GROUP_WIDTH), F32),
        pltpu.VMEM((SEQ + SWA_WINDOW, LANES), BF16),
        pltpu.VMEM((SEQ + SWA_WINDOW, LANES), BF16),
        pltpu.VMEM((N_HEADS, T, LANES), F32),
        pltpu.VMEM((N_HEADS, T, LANES), F32),
        pltpu.VMEM((2, N_HEADS, T, T), F32),
        pltpu.VMEM((T, D_MODEL), BF16),
    ]
    return pl.pallas_call(
        functools.partial(_mixer_kernel, layer=layer),
        grid=(BATCH, n_t),
        in_specs=in_specs,
        out_specs=pl.BlockSpec((1, T, D_MODEL), lambda b, i: (b, i, 0)),
        out_shape=jax.ShapeDtypeStruct((BATCH, SEQ, D_MODEL), F32),
        scratch_shapes=scratch,
        compiler_params=pltpu.CompilerParams(
            dimension_semantics=("arbitrary", "arbitrary"),
            vmem_limit_bytes=VMEM_LIMIT),
        name="mixer",
    )(x, mod, gpre, win, bfp, cw, cb, clg, clb, pww, pwb, sinks,
      slg, slb, sw, sb, gg, wout, gpost)


def _ffn_kernel(x_ref, mod_ref, gpre_ref, wup_ref, cw_ref, cb_ref, wdn_ref, gpost_ref,
                out_ref, car_scr, act_scr):
    T = T_FFN
    i = pl.program_id(1)

    @pl.when(i == 0)
    def _():
        car_scr[...] = jnp.zeros_like(car_scr)

    x = x_ref[0]
    mod = mod_ref[0]
    sh2, sc2, ga2 = mod[3:4], mod[4:5], mod[5:6]
    h = (_rms(x, gpre_ref[...]) * (1.0 + sc2) + sh2).astype(BF16)

    def conv_half(col0, slot):
        u = _mm(h, wup_ref[:, col0:col0 + FF_CHUNK])
        ext = jnp.concatenate([car_scr[slot], u], axis=0)
        back1 = pltpu.roll(ext, 1, 0)[SUBLANES:, :]
        back2 = pltpu.roll(ext, 2, 0)[SUBLANES:, :]
        car_scr[slot] = u[T - SUBLANES:, :]
        w = cw_ref[:, col0:col0 + FF_CHUNK]
        return w[2:3] * u + w[1:2] * back1 + w[0:1] * back2 + cb_ref[:, col0:col0 + FF_CHUNK]

    for c in range(N_FF_CHUNKS):
        g = conv_half(c * FF_CHUNK, 2 * c)
        v = conv_half(D_FF + c * FF_CHUNK, 2 * c + 1)
        act_scr[:, c * FF_CHUNK:(c + 1) * FF_CHUNK] = (g * _sigmoid(g) * v).astype(BF16)

    y = _mm(act_scr[...], wdn_ref[...])
    out_ref[0] = x + ga2 * _rms(y, gpost_ref[...])


def _ffn(layer, x, mod, gpre, wup, cw, cb, wdn, gpost):
    T = T_FFN
    n_t = SEQ // T
    blk = functools.partial(_layer_block, layer=layer)
    return pl.pallas_call(
        _ffn_kernel,
        grid=(BATCH, n_t),
        in_specs=[
            pl.BlockSpec((1, T, D_MODEL), lambda b, i: (b, i, 0)),
            pl.BlockSpec((None, 1, 6, D_MODEL), lambda b, i: (layer, b, 0, 0)),
            blk((1, D_MODEL)),
            blk((D_MODEL, 2 * D_FF)),
            blk((3, 2 * D_FF)),
            blk((1, 2 * D_FF)),
            blk((D_FF, D_MODEL)),
            blk((1, D_MODEL)),
        ],
        out_specs=pl.BlockSpec((1, T, D_MODEL), lambda b, i: (b, i, 0)),
        out_shape=jax.ShapeDtypeStruct((BATCH, SEQ, D_MODEL), F32),
        scratch_shapes=[
            pltpu.VMEM((2 * N_FF_CHUNKS, SUBLANES, FF_CHUNK), F32),
            pltpu.VMEM((T, D_FF), BF16),
        ],
        compiler_params=pltpu.CompilerParams(
            dimension_semantics=("arbitrary", "arbitrary"),
            vmem_limit_bytes=VMEM_LIMIT),
        name="ffn",
    )(x, mod, gpre, wup, cw, cb, wdn, gpost)


def _align_w_in(w_in):
    o_conv = 3 * GROUP_WIDTH + N_HEADS
    o_swa = o_conv + 2 * GROUP_WIDTH
    o_sgu = o_swa + GROUP_WIDTH + 2 * LANES
    swa_q = w_in[:, :, o_swa:o_swa + GROUP_WIDTH].reshape(DEPTH, D_MODEL, N_HEADS, HEAD_DIM)
    swa_q = swa_q[:, :, list(SWA_HEAD_ORDER), :].reshape(DEPTH, D_MODEL, GROUP_WIDTH)
    gate = jnp.pad(w_in[:, :, 3 * GROUP_WIDTH:o_conv], ((0, 0), (0, 0), (0, LANES - N_HEADS)))
    return jnp.concatenate([
        w_in[:, :, 0:3 * GROUP_WIDTH],
        w_in[:, :, o_conv:o_swa],
        swa_q,
        w_in[:, :, o_swa + GROUP_WIDTH:o_sgu],
        w_in[:, :, o_sgu:o_sgu + 2 * GROUP_WIDTH],
        gate,
    ], axis=-1).astype(BF16)


def _permute_swa_group(a, axis):
    lo, hi = 2 * GROUP_WIDTH, 3 * GROUP_WIDTH
    seg = lax.slice_in_dim(a, lo, hi, axis=axis)
    shape = seg.shape
    split = shape[:axis] + (N_HEADS, HEAD_DIM) + shape[axis + 1:]
    seg = jnp.take(seg.reshape(split), jnp.array(SWA_HEAD_ORDER), axis=axis).reshape(shape)
    return jnp.concatenate([lax.slice_in_dim(a, 0, lo, axis=axis), seg,
                            lax.slice_in_dim(a, hi, a.shape[axis], axis=axis)], axis=axis)


def kernel(x, c, w_ada, b_ada, g_pre_mix, g_post_mix, g_pre_ffn, g_post_ffn, w_in, b_fgate,
           conv_w, conv_b, conv_ln_g, conv_ln_b, conv_pw_w, conv_pw_b, swa_sinks,
           sgu_ln_g, sgu_ln_b, sgu_w, sgu_b, g_group, w_out,
           ffn_w_up, ffn_conv_w, ffn_conv_b, ffn_w_down):
    mod = _ada(c, w_ada, b_ada).reshape(DEPTH, BATCH, 6, D_MODEL)

    def rows(a):
        return a[:, None, :]

    win = _align_w_in(w_in)
    bfp = jnp.broadcast_to(
        jnp.pad(b_fgate, ((0, 0), (0, 2 * SUBLANES - N_HEADS)))[:, :, None],
        (DEPTH, 2 * SUBLANES, LANES)).astype(F32)
    pww = conv_pw_w.astype(BF16)
    sb = jnp.repeat(jnp.swapaxes(sgu_b, 1, 2), HEAD_DIM, axis=2)
    gg = rows(_permute_swa_group(g_group, 1))
    wout = _permute_swa_group(w_out, 1).astype(BF16)
    wup = ffn_w_up.astype(BF16)
    wdn = ffn_w_down.astype(BF16)

    for l in range(DEPTH):
        x = _mixer(l, x, mod, rows(g_pre_mix), win, bfp,
                   conv_w, rows(conv_b), rows(conv_ln_g), rows(conv_ln_b),
                   pww, rows(conv_pw_b), swa_sinks,
                   rows(sgu_ln_g), rows(sgu_ln_b), sgu_w, sb,
                   gg, wout, rows(g_post_mix))
        x = _ffn(l, x, mod, rows(g_pre_ffn), wup, ffn_conv_w, rows(ffn_conv_b),
                 wdn, rows(g_post_ffn))
    return x
```
